```python
import math
import jax, jax.numpy as jnp
from jax import lax
import numpy as np

D_MODEL = 2048
BATCH = 2
SEQ = 4096
DEPTH = 4

N_MIXERS = 2
N_A_LAYERS = (DEPTH + 1) // 2
N_B_LAYERS = DEPTH // 2

SWA_HEADS = 32
SWA_KV_HEADS = 4
SWA_HEAD_DIM = D_MODEL // SWA_HEADS
SWA_GROUP = SWA_HEADS // SWA_KV_HEADS
SWA_WINDOW = 128
SWA_BLOCK = SWA_WINDOW
SWA_QKV_COLS = (SWA_HEADS + 2 * SWA_KV_HEADS) * SWA_HEAD_DIM

DIFF_HEADS = 16
DIFF_HEAD_DIM = D_MODEL // (2 * DIFF_HEADS)
DIFF_V_DIM = 2 * DIFF_HEAD_DIM
Q_BLOCK = 128

MEM_LEN = 256
X_HEADS = 4
X_HEAD_DIM = 128
X_WIDTH = X_HEADS * X_HEAD_DIM

D_FF = 5632
CONV_WIDTH = 3

RMS_EPS = 1e-6
NEG_INF = -1e30

kernel_name = "hybrid_swa_sink_diffattn_memxattn_convffn"


def rmsnorm(x, g):
    xf = x.astype(jnp.float32)
    y = xf * lax.rsqrt(jnp.mean(xf * xf, axis=-1, keepdims=True) + RMS_EPS)
    return (y * g.astype(jnp.float32)).astype(x.dtype)


def alibi_slopes(n_heads):
    return jnp.asarray(np.array([2.0 ** (-8.0 * (h + 1) / n_heads) for h in range(n_heads)], dtype=np.float32))


def swa_sink_attention(h, w_qkv, w_o, sinks):
    B, S, _ = h.shape
    nb = S // SWA_BLOCK
    qkv = h @ w_qkv
    q_cols = SWA_HEADS * SWA_HEAD_DIM
    kv_cols = SWA_KV_HEADS * SWA_HEAD_DIM
    q, k, v = jnp.split(qkv, [q_cols, q_cols + kv_cols], axis=-1)
    q = q.reshape(B, nb, SWA_BLOCK, SWA_KV_HEADS, SWA_GROUP, SWA_HEAD_DIM)
    k = k.reshape(B, nb, SWA_BLOCK, SWA_KV_HEADS, SWA_HEAD_DIM)
    v = v.reshape(B, nb, SWA_BLOCK, SWA_KV_HEADS, SWA_HEAD_DIM)

    def band(t):
        prev = jnp.pad(t, ((0, 0), (1, 0), (0, 0), (0, 0), (0, 0)))[:, :-1]
        return jnp.concatenate([prev, t], axis=2)

    kb, vb = band(k), band(v)
    scores = jnp.einsum('bnqkgd,bnskd->bnkgqs', q, kb).astype(jnp.float32) * (SWA_HEAD_DIM ** -0.5)

    qi = jnp.arange(SWA_BLOCK)[:, None]
    sj = jnp.arange(2 * SWA_BLOCK)[None, :]
    dist = qi + SWA_BLOCK - sj
    in_window = (dist >= 0) & (dist < SWA_WINDOW)
    key_exists = (jnp.arange(nb)[:, None] * SWA_BLOCK + sj - SWA_BLOCK) >= 0
    mask = in_window[None] & key_exists[:, None, :]

    slopes = alibi_slopes(SWA_HEADS).reshape(SWA_KV_HEADS, SWA_GROUP)
    bias = -slopes[:, :, None, None] * dist.astype(jnp.float32)[None, None]
    scores = jnp.where(mask[None, :, None, None], scores + bias[None, None], NEG_INF)

    sink = sinks.astype(jnp.float32).reshape(SWA_KV_HEADS, SWA_GROUP)[None, None, :, :, None, None]
    sink = jnp.broadcast_to(sink, scores.shape[:-1] + (1,))
    probs = jax.nn.softmax(jnp.concatenate([scores, sink], axis=-1), axis=-1)[..., :-1]
    out = jnp.einsum('bnkgqs,bnskd->bnqkgd', probs.astype(vb.dtype), vb)
    return out.reshape(B, S, SWA_HEADS * SWA_HEAD_DIM) @ w_o


def diff_attention(h, w_qkv, w_o, lq1, lk1, lq2, lk2, subln_g, lambda_init):
    B, S, _ = h.shape
    qkv = h @ w_qkv
    q, k, v = jnp.split(qkv, 3, axis=-1)
    q = q.reshape(B, S, DIFF_HEADS, 2, DIFF_HEAD_DIM)
    k = k.reshape(B, S, DIFF_HEADS, 2, DIFF_HEAD_DIM)
    v = v.reshape(B, S, DIFF_HEADS, DIFF_V_DIM)
    lam = (jnp.exp(jnp.sum(lq1.astype(jnp.float32) * lk1.astype(jnp.float32)))
           - jnp.exp(jnp.sum(lq2.astype(jnp.float32) * lk2.astype(jnp.float32)))
           + lambda_init)
    slopes = alibi_slopes(DIFF_HEADS)
    nq = S // Q_BLOCK
    qb = q.reshape(B, nq, Q_BLOCK, DIFF_HEADS, 2, DIFF_HEAD_DIM).transpose(1, 0, 2, 3, 4, 5)
    kpos = jnp.arange(S)
    scale = DIFF_HEAD_DIM ** -0.5

    def block(args):
        qblk, i = args
        qpos = i * Q_BLOCK + jnp.arange(Q_BLOCK)
        dist = qpos[:, None] - kpos[None, :]
        s = jnp.einsum('bqhmd,bshmd->bhmqs', qblk, k).astype(jnp.float32) * scale
        s = s - slopes[None, :, None, None, None] * dist.astype(jnp.float32)
        s = jnp.where(dist >= 0, s, NEG_INF)
        p = jax.nn.softmax(s, axis=-1)
        a = p[:, :, 0] - lam * p[:, :, 1]
        return jnp.einsum('bhqs,bshe->bqhe', a.astype(v.dtype), v)

    o = lax.map(block, (qb, jnp.arange(nq)))
    o = o.transpose(1, 0, 2, 3, 4).reshape(B, S, DIFF_HEADS, DIFF_V_DIM)
    o = rmsnorm(o, subln_g) * (1.0 - lambda_init)
    return o.reshape(B, S, DIFF_HEADS * DIFF_V_DIM) @ w_o


def mem_cross_attention(h, mem_n, w_q, w_kv, w_o):
    B, S, _ = h.shape
    q = (h @ w_q).reshape(B, S, X_HEADS, X_HEAD_DIM)
    k, v = jnp.split(mem_n @ w_kv, 2, axis=-1)
    k = k.reshape(B, -1, X_HEADS, X_HEAD_DIM)
    v = v.reshape(B, -1, X_HEADS, X_HEAD_DIM)
    s = jnp.einsum('bqhd,bmhd->bhqm', q, k).astype(jnp.float32) * (X_HEAD_DIM ** -0.5)
    p = jax.nn.softmax(s, axis=-1)
    o = jnp.einsum('bhqm,bmhd->bqhd', p.astype(v.dtype), v).reshape(B, S, X_WIDTH)
    return o @ w_o


def conv_ffn(h, w_up, conv_w, conv_b, w_down):
    u = h @ w_up
    u = lax.conv_general_dilated(
        u, conv_w[:, None, :], window_strides=(1,), padding=[(CONV_WIDTH - 1, 0)],
        dimension_numbers=('NWC', 'WIO', 'NWC'), feature_group_count=2 * D_FF) + conv_b
    gate, up = jnp.split(u, 2, axis=-1)
    return (jax.nn.silu(gate) * up) @ w_down


def setup_inputs(seed: int = 0) -> dict:
    key = jax.random.key(seed)
    ks = iter(jax.random.split(key, 32))
    f32 = jnp.float32

    def normal(shape, scale):
        return jax.random.normal(next(ks), shape, f32) * scale

    def gain(shape):
        return 1.0 + normal(shape, 0.02)

    out_scale = (2.0 * DEPTH) ** -0.5
    return {
        "x": normal((BATCH, SEQ, D_MODEL), 1.0),
        "mem": normal((BATCH, MEM_LEN, D_MODEL), 1.0),
        "norm_mix": gain((DEPTH, D_MODEL)),
        "norm_cross": gain((DEPTH, D_MODEL)),
        "norm_mem": gain((DEPTH, D_MODEL)),
        "norm_ffn": gain((DEPTH, D_MODEL)),
        "norm_final": gain((D_MODEL,)),
        "swa_w_qkv": normal((N_A_LAYERS, D_MODEL, SWA_QKV_COLS), D_MODEL ** -0.5),
        "swa_w_o": normal((N_A_LAYERS, SWA_HEADS * SWA_HEAD_DIM, D_MODEL), (SWA_HEADS * SWA_HEAD_DIM) ** -0.5 * out_scale),
        "swa_sinks": normal((N_A_LAYERS, SWA_HEADS), 0.5),
        "diff_w_qkv": normal((N_B_LAYERS, D_MODEL, 3 * D_MODEL), D_MODEL ** -0.5),
        "diff_w_o": normal((N_B_LAYERS, DIFF_HEADS * DIFF_V_DIM, D_MODEL), (DIFF_HEADS * DIFF_V_DIM) ** -0.5 * out_scale),
        "diff_lambda_q1": normal((N_B_LAYERS, DIFF_HEAD_DIM), 0.1),
        "diff_lambda_k1": normal((N_B_LAYERS, DIFF_HEAD_DIM), 0.1),
        "diff_lambda_q2": normal((N_B_LAYERS, DIFF_HEAD_DIM), 0.1),
        "diff_lambda_k2": normal((N_B_LAYERS, DIFF_HEAD_DIM), 0.1),
        "diff_subln": gain((N_B_LAYERS, DIFF_V_DIM)),
        "x_w_q": normal((DEPTH, D_MODEL, X_WIDTH), D_MODEL ** -0.5),
        "x_w_kv": normal((DEPTH, D_MODEL, 2 * X_WIDTH), D_MODEL ** -0.5),
        "x_w_o": normal((DEPTH, X_WIDTH, D_MODEL), X_WIDTH ** -0.5 * out_scale),
        "ffn_w_up": normal((DEPTH, D_MODEL, 2 * D_FF), D_MODEL ** -0.5),
        "ffn_conv_w": normal((DEPTH, CONV_WIDTH, 2 * D_FF), CONV_WIDTH ** -0.5),
        "ffn_conv_b": normal((DEPTH, 2 * D_FF), 0.02),
        "ffn_w_down": normal((DEPTH, D_FF, D_MODEL), D_FF ** -0.5 * out_scale),
    }


def reference(x, mem, norm_mix, norm_cross, norm_mem, norm_ffn, norm_final,
              swa_w_qkv, swa_w_o, swa_sinks,
              diff_w_qkv, diff_w_o, diff_lambda_q1, diff_lambda_k1, diff_lambda_q2, diff_lambda_k2, diff_subln,
              x_w_q, x_w_kv, x_w_o,
              ffn_w_up, ffn_conv_w, ffn_conv_b, ffn_w_down):
    h = x
    for i in range(DEPTH):
        j = i // N_MIXERS
        hn = rmsnorm(h, norm_mix[i])
        if i % N_MIXERS == 0:
            h = h + swa_sink_attention(hn, swa_w_qkv[j], swa_w_o[j], swa_sinks[j])
        else:
            lambda_init = 0.8 - 0.6 * math.exp(-0.3 * i)
            h = h + diff_attention(hn, diff_w_qkv[j], diff_w_o[j], diff_lambda_q1[j], diff_lambda_k1[j],
                                   diff_lambda_q2[j], diff_lambda_k2[j], diff_subln[j], lambda_init)
        mem_n = rmsnorm(mem, norm_mem[i])
        h = h + mem_cross_attention(rmsnorm(h, norm_cross[i]), mem_n, x_w_q[i], x_w_kv[i], x_w_o[i])
        h = h + conv_ffn(rmsnorm(h, norm_ffn[i]), ffn_w_up[i], ffn_conv_w[i], ffn_conv_b[i], ffn_w_down[i])
    return rmsnorm(h, norm_final)
```

```python
import functools
import math

import numpy as np
import jax
import jax.numpy as jnp
from jax import lax
from jax.experimental import pallas as pl
from jax.experimental.pallas import tpu as pltpu

F32 = jnp.float32
BF16 = jnp.bfloat16

D_MODEL = 2048
DEPTH = 4
N_MIXERS = 2

SWA_HEADS = 32
SWA_KV_HEADS = 4
SWA_HEAD_DIM = 64
SWA_GROUP = SWA_HEADS // SWA_KV_HEADS
SWA_BLOCK = 128
SWA_Q_COLS = SWA_HEADS * SWA_HEAD_DIM
SWA_KV_COLS = SWA_KV_HEADS * SWA_HEAD_DIM

DIFF_HEADS = 16
DIFF_HEAD_DIM = 64
DIFF_V_DIM = 128

X_HEADS = 4
X_HEAD_DIM = 128
X_WIDTH = X_HEADS * X_HEAD_DIM

D_FF = 5632
RMS_EPS = 1e-6
NEG_INF = -1e30

LANES = 128
VMEM_LIMIT = 56 * 1024 * 1024


def _alibi_slopes(n_heads):
    return np.array([2.0 ** (-8.0 * (h + 1) / n_heads) for h in range(n_heads)], dtype=np.float32)


def _params(semantics):
    return pltpu.CompilerParams(dimension_semantics=semantics, vmem_limit_bytes=VMEM_LIMIT)


def _rms_rows(x, g):
    ms = jnp.mean(x * x, axis=-1, keepdims=True)
    return x * lax.rsqrt(ms + RMS_EPS) * g


def _norm_to_bf16(x_ref, g_ref, xn_ref, rows, chunk=256):
    g = g_ref[...]

    def body(c, carry):
        r = pl.multiple_of(c * chunk, chunk)
        xn_ref[pl.ds(r, chunk), :] = _rms_rows(x_ref[pl.ds(r, chunk), :], g).astype(BF16)
        return carry

    lax.fori_loop(0, rows // chunk, body, 0)


def _norm_matmul_kernel(x_ref, g_ref, w_ref, o_ref, xn_ref, *, tm):
    @pl.when(pl.program_id(1) == 0)
    def _():
        _norm_to_bf16(x_ref, g_ref, xn_ref, tm)

    o_ref[...] = jnp.dot(xn_ref[...], w_ref[...], preferred_element_type=F32).astype(o_ref.dtype)


def norm_matmul(x, g, w, *, tm, tn, name):
    m, d = x.shape
    n = w.shape[1]
    return pl.pallas_call(
        functools.partial(_norm_matmul_kernel, tm=tm),
        grid=(m // tm, n // tn),
        in_specs=[
            pl.BlockSpec((tm, d), lambda i, j: (i, 0)),
            pl.BlockSpec((1, d), lambda i, j: (0, 0)),
            pl.BlockSpec((d, tn), lambda i, j: (0, j)),
        ],
        out_specs=pl.BlockSpec((tm, tn), lambda i, j: (i, j)),
        out_shape=jax.ShapeDtypeStruct((m, n), BF16),
        scratch_shapes=[pltpu.VMEM((tm, d), BF16)],
        compiler_params=_params(("parallel", "arbitrary")),
        name=name,
    )(x, g.reshape(1, d), w)


def _matmul_residual_kernel(a_ref, w_ref, h_ref, o_ref):
    o_ref[...] = h_ref[...] + jnp.dot(a_ref[...], w_ref[...], preferred_element_type=F32)


def matmul_residual(a, w, h, *, tm, tn, name):
    m, k = a.shape
    n = w.shape[1]
    return pl.pallas_call(
        _matmul_residual_kernel,
        grid=(m // tm, n // tn),
        in_specs=[
            pl.BlockSpec((tm, k), lambda i, j: (i, 0)),
            pl.BlockSpec((k, tn), lambda i, j: (0, j)),
            pl.BlockSpec((tm, tn), lambda i, j: (i, j)),
        ],
        out_specs=pl.BlockSpec((tm, tn), lambda i, j: (i, j)),
        out_shape=jax.ShapeDtypeStruct((m, n), F32),
        compiler_params=_params(("parallel", "arbitrary")),
        name=name,
    )(a, w, h)


def _swa_kernel(sinks_ref, q_ref, kp_ref, kc_ref, vp_ref, vc_ref, o_ref, *, blocks_per_seq):
    n = pl.program_id(0)
    first = (n % blocks_per_seq) == 0
    blk = SWA_BLOCK
    qi = lax.broadcasted_iota(jnp.int32, (blk, 2 * blk), 0)
    sj = lax.broadcasted_iota(jnp.int32, (blk, 2 * blk), 1)
    dist = qi + blk - sj
    first_key = jnp.where(first, blk, 0)
    mask = (dist >= 0) & (dist < blk) & (sj >= first_key)
    neg_dist = -dist.astype(F32)
    low = lax.broadcasted_iota(jnp.int32, (2 * blk, LANES), 1) < SWA_HEAD_DIM
    slopes = _alibi_slopes(SWA_HEADS)

    def halves(prev_ref, cur_ref, c):
        slab = (c // 2) * LANES
        t = jnp.concatenate([prev_ref[:, slab:slab + LANES], cur_ref[:, slab:slab + LANES]], axis=0).astype(F32)
        t_sw = pltpu.roll(t, SWA_HEAD_DIM, 1)
        lo_src, hi_src = (t, t_sw) if c % 2 == 0 else (t_sw, t)
        return (jnp.where(low, lo_src, 0.0).astype(BF16), jnp.where(low, 0.0, hi_src).astype(BF16))

    contract_lanes = (((1,), (1,)), ((), ()))
    for c in range(SWA_KV_HEADS):
        k_lo, k_hi = halves(kp_ref, kc_ref, c)
        v_lo, v_hi = halves(vp_ref, vc_ref, c)
        for p in range(SWA_GROUP // 2):
            col = (c * (SWA_GROUP // 2) + p) * LANES
            q_pair = q_ref[:, col:col + LANES] * jnp.asarray(SWA_HEAD_DIM ** -0.5, BF16)
            probs = []
            for half, k_ext in enumerate((k_lo, k_hi)):
                h = c * SWA_GROUP + 2 * p + half
                s = lax.dot_general(q_pair, k_ext, contract_lanes, preferred_element_type=F32)
                s = jnp.where(mask, s + float(slopes[h]) * neg_dist, NEG_INF)
                sink = sinks_ref[h]
                m = jnp.maximum(jnp.max(s, axis=-1, keepdims=True), sink)
                e = jnp.exp(s - m)
                denom = jnp.sum(e, axis=-1, keepdims=True) + jnp.exp(sink - m)
                probs.append((e / denom).astype(BF16))
            o_pair = (jnp.dot(probs[0], v_lo, preferred_element_type=F32)
                      + jnp.dot(probs[1], v_hi, preferred_element_type=F32))
            o_ref[:, col:col + LANES] = o_pair.astype(o_ref.dtype)


def swa_attention(qkv, sinks, *, seq, name):
    m = qkv.shape[0]
    blk = SWA_BLOCK
    bps = seq // blk
    kcol = SWA_Q_COLS // SWA_KV_COLS
    vcol = kcol + 1

    def prev(n):
        return jnp.maximum(n - 1, 0)

    return pl.pallas_call(
        functools.partial(_swa_kernel, blocks_per_seq=bps),
        grid=(m // blk,),
        in_specs=[
            pl.BlockSpec(memory_space=pltpu.SMEM),
            pl.BlockSpec((blk, SWA_Q_COLS), lambda n: (n, 0)),
            pl.BlockSpec((blk, SWA_KV_COLS), lambda n: (prev(n), kcol)),
            pl.BlockSpec((blk, SWA_KV_COLS), lambda n: (n, kcol)),
            pl.BlockSpec((blk, SWA_KV_COLS), lambda n: (prev(n), vcol)),
            pl.BlockSpec((blk, SWA_KV_COLS), lambda n: (n, vcol)),
        ],
        out_specs=pl.BlockSpec((blk, SWA_Q_COLS), lambda n: (n, 0)),
        out_shape=jax.ShapeDtypeStruct((m, SWA_Q_COLS), BF16),
        compiler_params=_params(("parallel",)),
        name=name,
    )(sinks, qkv, qkv, qkv, qkv, qkv)


def _diff_attn_kernel(slopes_ref, q_ref, k_ref, v_ref, lq1_ref, lk1_ref, lq2_ref, lk2_ref, sg_ref, o_ref,
                      m_ref, l_ref, acc_ref, *, tq, lambda_init):
    h = pl.program_id(1)
    qi = pl.program_id(2)
    slope = slopes_ref[h]

    q = q_ref[...] * jnp.asarray(DIFF_HEAD_DIM ** -0.5, BF16)
    low = lax.broadcasted_iota(jnp.int32, (tq, LANES), 1) < DIFF_HEAD_DIM
    zero = jnp.zeros_like(q)
    qq = jnp.concatenate([jnp.where(low, q, zero), jnp.where(low, zero, q)], axis=0)

    row = lax.broadcasted_iota(jnp.int32, (2 * tq, tq), 0)
    col = lax.broadcasted_iota(jnp.int32, (2 * tq, tq), 1)
    rel = col - jnp.where(row >= tq, row - tq, row)
    bias = slope * rel.astype(F32)

    m_ref[...] = jnp.full(m_ref.shape, NEG_INF, F32)
    l_ref[...] = jnp.zeros(l_ref.shape, F32)
    acc_ref[...] = jnp.zeros(acc_ref.shape, F32)
    contract_lanes = (((1,), (1,)), ((), ()))

    def step(kj, diagonal):
        k0 = pl.multiple_of(kj * tq, tq)
        kb = k_ref[pl.ds(k0, tq), :]
        vb = v_ref[pl.ds(k0, tq), :]
        s = lax.dot_general(qq, kb, contract_lanes, preferred_element_type=F32) + bias
        if diagonal:
            s = jnp.where(rel <= 0, s, NEG_INF)
            shift = 0.0
        else:
            shift = slope * ((kj - qi) * tq).astype(F32)
        m_old = m_ref[...]
        m_new = jnp.maximum(m_old, jnp.max(s, axis=-1, keepdims=True) + shift)
        p = jnp.exp(s - (m_new - shift))
        alpha = jnp.exp(m_old - m_new)
        l_ref[...] = alpha * l_ref[...] + jnp.sum(p, axis=-1, keepdims=True)
        acc_ref[...] = alpha * acc_ref[...] + jnp.dot(p.astype(BF16), vb, preferred_element_type=F32)
        m_ref[...] = m_new

    def body(kj, carry):
        step(kj, False)
        return carry

    lax.fori_loop(0, qi, body, 0)
    step(qi, True)

    lam = (jnp.exp(jnp.sum(lq1_ref[...] * lk1_ref[...], axis=-1, keepdims=True))
           - jnp.exp(jnp.sum(lq2_ref[...] * lk2_ref[...], axis=-1, keepdims=True))
           + lambda_init)
    o = acc_ref[...] / l_ref[...]
    o = o[:tq] - lam * o[tq:]
    o = _rms_rows(o, sg_ref[...]) * (1.0 - lambda_init)
    o_ref[...] = o.astype(o_ref.dtype)


def diff_attention(qkv, lq1, lk1, lq2, lk2, subln, *, batch, seq, tq, lambda_init, name):
    m = qkv.shape[0]
    nq = seq // tq
    slopes = jnp.asarray(_alibi_slopes(DIFF_HEADS))
    vec = lambda a: a.reshape(1, -1).astype(F32)
    small = lambda width: pl.BlockSpec((1, width), lambda b, h, i: (0, 0))
    return pl.pallas_call(
        functools.partial(_diff_attn_kernel, tq=tq, lambda_init=lambda_init),
        grid=(batch, DIFF_HEADS, nq),
        in_specs=[
            pl.BlockSpec(memory_space=pltpu.SMEM),
            pl.BlockSpec((tq, DIFF_V_DIM), lambda b, h, i: (b * nq + i, h)),
            pl.BlockSpec((seq, DIFF_V_DIM), lambda b, h, i: (b, DIFF_HEADS + h)),
            pl.BlockSpec((seq, DIFF_V_DIM), lambda b, h, i: (b, 2 * DIFF_HEADS + h)),
            small(DIFF_HEAD_DIM), small(DIFF_HEAD_DIM), small(DIFF_HEAD_DIM), small(DIFF_HEAD_DIM),
            small(DIFF_V_DIM),
        ],
        out_specs=pl.BlockSpec((tq, DIFF_V_DIM), lambda b, h, i: (b * nq + i, h)),
        out_shape=jax.ShapeDtypeStruct((m, DIFF_HEADS * DIFF_V_DIM), BF16),
        scratch_shapes=[
            pltpu.VMEM((2 * tq, 1), F32),
            pltpu.VMEM((2 * tq, 1), F32),
            pltpu.VMEM((2 * tq, DIFF_V_DIM), F32),
        ],
        compiler_params=_params(("parallel", "parallel", "arbitrary")),
        name=name,
    )(slopes, qkv, qkv, qkv, vec(lq1), vec(lk1), vec(lq2), vec(lk2), vec(subln))


def _xattn_kernel(h_ref, g_ref, wq_ref, k_ref, v_ref, wo_ref, o_ref):
    x = h_ref[...]
    xn = _rms_rows(x, g_ref[...]).astype(BF16)
    q = jnp.dot(xn, wq_ref[...], preferred_element_type=F32).astype(BF16)
    contract_lanes = (((1,), (1,)), ((), ()))
    outs = []
    for a in range(X_HEADS):
        sl = slice(a * X_HEAD_DIM, (a + 1) * X_HEAD_DIM)
        s = lax.dot_general(q[:, sl], k_ref[:, sl], contract_lanes, preferred_element_type=F32)
        s = s * (X_HEAD_DIM ** -0.5)
        e = jnp.exp(s - jnp.max(s, axis=-1, keepdims=True))
        p = (e / jnp.sum(e, axis=-1, keepdims=True)).astype(BF16)
        outs.append(jnp.dot(p, v_ref[:, sl], preferred_element_type=F32).astype(BF16))
    o = jnp.concatenate(outs, axis=-1)
    o_ref[...] = x + jnp.dot(o, wo_ref[...], preferred_element_type=F32)


def cross_attention(h, g, wq, memkv, wo, *, seq, tm, name):
    m, d = h.shape
    mem_len = memkv.shape[0] // (m // seq)
    per_seq = seq // tm
    return pl.pallas_call(
        _xattn_kernel,
        grid=(m // tm,),
        in_specs=[
            pl.BlockSpec((tm, d), lambda i: (i, 0)),
            pl.BlockSpec((1, d), lambda i: (0, 0)),
            pl.BlockSpec((d, X_WIDTH), lambda i: (0, 0)),
            pl.BlockSpec((mem_len, X_WIDTH), lambda i: (i // per_seq, 0)),
            pl.BlockSpec((mem_len, X_WIDTH), lambda i: (i // per_seq, 1)),
            pl.BlockSpec((X_WIDTH, d), lambda i: (0, 0)),
        ],
        out_specs=pl.BlockSpec((tm, d), lambda i: (i, 0)),
        out_shape=jax.ShapeDtypeStruct((m, d), F32),
        compiler_params=_params(("parallel",)),
        name=name,
    )(h, g.reshape(1, d), wq, memkv, memkv, wo)


CONV_HALO = 8


def _ffn_up_kernel(x_ref, g_ref, wg_ref, wu_ref, cwg_ref, cwu_ref, cbg_ref, cbu_ref, o_ref,
                   xn_ref, ub_ref, carry_ref, *, tm, rc, tiles_per_seq):
    i = pl.program_id(0)
    j = pl.program_id(1)

    @pl.when(j == 0)
    def _():
        _norm_to_bf16(x_ref, g_ref, xn_ref, tm)

    first = (i % tiles_per_seq) == 0

    @pl.when(first)
    def _():
        ub_ref[:, 0:CONV_HALO, :] = jnp.zeros((2, CONV_HALO, ub_ref.shape[2]), F32)

    @pl.when(jnp.logical_not(first))
    def _():
        ub_ref[:, 0:CONV_HALO, :] = carry_ref[j]

    taps = ((cwg_ref[...], cbg_ref[...]), (cwu_ref[...], cbu_ref[...]))
    w_refs = (wg_ref, wu_ref)
    for r in range(tm // rc):
        xs = xn_ref[r * rc:(r + 1) * rc, :]
        lo = CONV_HALO + r * rc
        conv = []
        for t in range(2):
            ub_ref[t, lo:lo + rc, :] = jnp.dot(xs, w_refs[t][...], preferred_element_type=F32)
            cw, cb = taps[t]
            conv.append(cw[2:3] * ub_ref[t, lo:lo + rc, :]
                        + cw[1:2] * ub_ref[t, lo - 1:lo - 1 + rc, :]
                        + cw[0:1] * ub_ref[t, lo - 2:lo - 2 + rc, :]
                        + cb)
        gate, up = conv
        o_ref[r * rc:(r + 1) * rc, :] = (gate / (1.0 + jnp.exp(-gate)) * up).astype(o_ref.dtype)

    carry_ref[j] = ub_ref[:, tm:tm + CONV_HALO, :]


def ffn_up(h, g, w_up, conv_w, conv_b, *, seq, tm, tn, rc, name):
    m, d = h.shape
    nj = D_FF // tn
    return pl.pallas_call(
        functools.partial(_ffn_up_kernel, tm=tm, rc=rc, tiles_per_seq=seq // tm),
        grid=(m // tm, nj),
        in_specs=[
            pl.BlockSpec((tm, d), lambda i, j: (i, 0)),
            pl.BlockSpec((1, d), lambda i, j: (0, 0)),
            pl.BlockSpec((d, tn), lambda i, j: (0, j)),
            pl.BlockSpec((d, tn), lambda i, j: (0, nj + j)),
            pl.BlockSpec((3, tn), lambda i, j: (0, j)),
            pl.BlockSpec((3, tn), lambda i, j: (0, nj + j)),
            pl.BlockSpec((1, tn), lambda i, j: (0, j)),
            pl.BlockSpec((1, tn), lambda i, j: (0, nj + j)),
        ],
        out_specs=pl.BlockSpec((tm, tn), lambda i, j: (i, j)),
        out_shape=jax.ShapeDtypeStruct((m, D_FF), BF16),
        scratch_shapes=[
            pltpu.VMEM((tm, d), BF16),
            pltpu.VMEM((2, CONV_HALO + tm, tn), F32),
            pltpu.VMEM((nj, 2, CONV_HALO, tn), F32),
        ],
        compiler_params=_params(("arbitrary", "arbitrary")),
        name=name,
    )(h, g.reshape(1, d), w_up, w_up, conv_w, conv_w, conv_b.reshape(1, -1), conv_b.reshape(1, -1))


def _final_norm_kernel(x_ref, g_ref, o_ref):
    o_ref[...] = _rms_rows(x_ref[...], g_ref[...])


def final_norm(h, g, *, tm, name):
    m, d = h.shape
    return pl.pallas_call(
        _final_norm_kernel,
        grid=(m // tm,),
        in_specs=[pl.BlockSpec((tm, d), lambda i: (i, 0)), pl.BlockSpec((1, d), lambda i: (0, 0))],
        out_specs=pl.BlockSpec((tm, d), lambda i: (i, 0)),
        out_shape=jax.ShapeDtypeStruct((m, d), F32),
        compiler_params=_params(("parallel",)),
        name=name,
    )(h, g.reshape(1, d))


def kernel(x, mem, norm_mix, norm_cross, norm_mem, norm_ffn, norm_final, swa_w_qkv, swa_w_o, swa_sinks,
           diff_w_qkv, diff_w_o, diff_lambda_q1, diff_lambda_k1, diff_lambda_q2, diff_lambda_k2, diff_subln,
           x_w_q, x_w_kv, x_w_o, ffn_w_up, ffn_conv_w, ffn_conv_b, ffn_w_down):
    batch, seq, d = x.shape
    mem_len = mem.shape[1]
    h = x.reshape(batch * seq, d)
    mem2 = mem.reshape(batch * mem_len, d)
    bf = lambda w: w.astype(BF16)

    for i in range(DEPTH):
        j = i // N_MIXERS
        if i % N_MIXERS == 0:
            qkv = norm_matmul(h, norm_mix[i], bf(swa_w_qkv[j]), tm=1024, tn=1280, name=f"swa_qkv_{i}")
            attn = swa_attention(qkv, swa_sinks[j].astype(F32), seq=seq, name=f"swa_attn_{i}")
            h = matmul_residual(attn, bf(swa_w_o[j]), h, tm=1024, tn=1024, name=f"swa_out_{i}")
        else:
            lambda_init = 0.8 - 0.6 * math.exp(-0.3 * i)
            qkv = norm_matmul(h, norm_mix[i], bf(diff_w_qkv[j]), tm=1024, tn=1024, name=f"diff_qkv_{i}")
            attn = diff_attention(qkv, diff_lambda_q1[j], diff_lambda_k1[j], diff_lambda_q2[j],
                                  diff_lambda_k2[j], diff_subln[j], batch=batch, seq=seq, tq=512,
                                  lambda_init=lambda_init, name=f"diff_attn_{i}")
            h = matmul_residual(attn, bf(diff_w_o[j]), h, tm=1024, tn=1024, name=f"diff_out_{i}")

        memkv = norm_matmul(mem2, norm_mem[i], bf(x_w_kv[i]), tm=batch * mem_len, tn=2 * X_WIDTH,
                            name=f"mem_kv_{i}")
        h = cross_attention(h, norm_cross[i], bf(x_w_q[i]), memkv, bf(x_w_o[i]), seq=seq, tm=512,
                            name=f"xattn_{i}")
        u = ffn_up(h, norm_ffn[i], bf(ffn_w_up[i]), ffn_conv_w[i], ffn_conv_b[i], seq=seq, tm=1024, tn=512,
                   rc=256, name=f"ffn_up_{i}")
        h = matmul_residual(u, bf(ffn_w_down[i]), h, tm=1024, tn=512, name=f"ffn_down_{i}")

    return final_norm(h, norm_final, tm=512, name="final_norm").reshape(batch, seq, d)
```

```python
import functools
import math

import numpy as np
import jax
import jax.numpy as jnp
from jax import lax
from jax.experimental import pallas as pl
from jax.experimental.pallas import tpu as pltpu

F32 = jnp.float32
BF16 = jnp.bfloat16

D_MODEL = 2048
DEPTH = 4
N_MIXERS = 2

SWA_HEADS = 32
SWA_KV_HEADS = 4
SWA_HEAD_DIM = 64
SWA_GROUP = SWA_HEADS // SWA_KV_HEADS
SWA_BLOCK = 128
SWA_Q_COLS = SWA_HEADS * SWA_HEAD_DIM
SWA_KV_COLS = SWA_KV_HEADS * SWA_HEAD_DIM

DIFF_HEADS = 16
DIFF_HEAD_DIM = 64
DIFF_V_DIM = 128

X_HEADS = 4
X_HEAD_DIM = 128
X_WIDTH = X_HEADS * X_HEAD_DIM

D_FF = 5632
RMS_EPS = 1e-6
NEG_INF = -1e30

LANES = 128
VMEM_LIMIT = 56 * 1024 * 1024


def _alibi_slopes(n_heads):
    return np.array([2.0 ** (-8.0 * (h + 1) / n_heads) for h in range(n_heads)], dtype=np.float32)


def _params(semantics):
    return pltpu.CompilerParams(dimension_semantics=semantics, vmem_limit_bytes=VMEM_LIMIT)


def _rms_rows(x, g):
    ms = jnp.mean(x * x, axis=-1, keepdims=True)
    return x * lax.rsqrt(ms + RMS_EPS) * g


def _norm_to_bf16(x_ref, g_ref, xn_ref, rows, chunk=256):
    g = g_ref[...]

    def body(c, carry):
        r = pl.multiple_of(c * chunk, chunk)
        xn_ref[pl.ds(r, chunk), :] = _rms_rows(x_ref[pl.ds(r, chunk), :], g).astype(BF16)
        return carry

    lax.fori_loop(0, rows // chunk, body, 0)


def _norm_matmul_kernel(x_ref, g_ref, w_ref, o_ref, xn_ref, *, tm):
    @pl.when(pl.program_id(1) == 0)
    def _():
        _norm_to_bf16(x_ref, g_ref, xn_ref, tm)

    o_ref[...] = jnp.dot(xn_ref[...], w_ref[...], preferred_element_type=F32).astype(o_ref.dtype)


def norm_matmul(x, g, w, *, tm, tn, name):
    m, d = x.shape
    n = w.shape[1]
    return pl.pallas_call(
        functools.partial(_norm_matmul_kernel, tm=tm),
        grid=(m // tm, n // tn),
        in_specs=[
            pl.BlockSpec((tm, d), lambda i, j: (i, 0)),
            pl.BlockSpec((1, d), lambda i, j: (0, 0)),
            pl.BlockSpec((d, tn), lambda i, j: (0, j)),
        ],
        out_specs=pl.BlockSpec((tm, tn), lambda i, j: (i, j)),
        out_shape=jax.ShapeDtypeStruct((m, n), BF16),
        scratch_shapes=[pltpu.VMEM((tm, d), BF16)],
        compiler_params=_params(("parallel", "arbitrary")),
        name=name,
    )(x, g.reshape(1, d), w)


def _matmul_residual_kernel(a_ref, w_ref, h_ref, o_ref):
    o_ref[...] = h_ref[...] + jnp.dot(a_ref[...], w_ref[...], preferred_element_type=F32)


def matmul_residual(a, w, h, *, tm, tn, name):
    m, k = a.shape
    n = w.shape[1]
    return pl.pallas_call(
        _matmul_residual_kernel,
        grid=(m // tm, n // tn),
        in_specs=[
            pl.BlockSpec((tm, k), lambda i, j: (i, 0)),
            pl.BlockSpec((k, tn), lambda i, j: (0, j)),
            pl.BlockSpec((tm, tn), lambda i, j: (i, j)),
        ],
        out_specs=pl.BlockSpec((tm, tn), lambda i, j: (i, j)),
        out_shape=jax.ShapeDtypeStruct((m, n), F32),
        compiler_params=_params(("parallel", "arbitrary")),
        name=name,
    )(a, w, h)


def _swa_kernel(sinks_ref, q_ref, kp_ref, kc_ref, vp_ref, vc_ref, o_ref, *, blocks_per_seq):
    n = pl.program_id(0)
    first = (n % blocks_per_seq) == 0
    blk = SWA_BLOCK
    qi = lax.broadcasted_iota(jnp.int32, (blk, 2 * blk), 0)
    sj = lax.broadcasted_iota(jnp.int32, (blk, 2 * blk), 1)
    dist = qi + blk - sj
    first_key = jnp.where(first, blk, 0)
    mask = (dist >= 0) & (dist < blk) & (sj >= first_key)
    neg_dist = -dist.astype(F32)
    low = lax.broadcasted_iota(jnp.int32, (2 * blk, LANES), 1) < SWA_HEAD_DIM
    slopes = _alibi_slopes(SWA_HEADS)

    def halves(prev_ref, cur_ref, c):
        slab = (c // 2) * LANES
        t = jnp.concatenate([prev_ref[:, slab:slab + LANES], cur_ref[:, slab:slab + LANES]], axis=0).astype(F32)
        t_sw = pltpu.roll(t, SWA_HEAD_DIM, 1)
        lo_src, hi_src = (t, t_sw) if c % 2 == 0 else (t_sw, t)
        return (jnp.where(low, lo_src, 0.0).astype(BF16), jnp.where(low, 0.0, hi_src).astype(BF16))

    contract_lanes = (((1,), (1,)), ((), ()))
    for c in range(SWA_KV_HEADS):
        k_lo, k_hi = halves(kp_ref, kc_ref, c)
        v_lo, v_hi = halves(vp_ref, vc_ref, c)
        for p in range(SWA_GROUP // 2):
            col = (c * (SWA_GROUP // 2) + p) * LANES
            q_pair = q_ref[:, col:col + LANES] * jnp.asarray(SWA_HEAD_DIM ** -0.5, BF16)
            probs = []
            for half, k_ext in enumerate((k_lo, k_hi)):
                h = c * SWA_GROUP + 2 * p + half
                s = lax.dot_general(q_pair, k_ext, contract_lanes, preferred_element_type=F32)
                s = jnp.where(mask, s + float(slopes[h]) * neg_dist, NEG_INF)
                sink = sinks_ref[h]
                m = jnp.maximum(jnp.max(s, axis=-1, keepdims=True), sink)
                e = jnp.exp(s - m)
                denom = jnp.sum(e, axis=-1, keepdims=True) + jnp.exp(sink - m)
                probs.append((e / denom).astype(BF16))
            o_pair = (jnp.dot(probs[0], v_lo, preferred_element_type=F32)
                      + jnp.dot(probs[1], v_hi, preferred_element_type=F32))
            o_ref[:, col:col + LANES] = o_pair.astype(o_ref.dtype)


def swa_attention(qkv, sinks, *, seq, name):
    m = qkv.shape[0]
    blk = SWA_BLOCK
    bps = seq // blk
    kcol = SWA_Q_COLS // SWA_KV_COLS
    vcol = kcol + 1

    def prev(n):
        return jnp.maximum(n - 1, 0)

    return pl.pallas_call(
        functools.partial(_swa_kernel, blocks_per_seq=bps),
        grid=(m // blk,),
        in_specs=[
            pl.BlockSpec(memory_space=pltpu.SMEM),
            pl.BlockSpec((blk, SWA_Q_COLS), lambda n: (n, 0)),
            pl.BlockSpec((blk, SWA_KV_COLS), lambda n: (prev(n), kcol)),
            pl.BlockSpec((blk, SWA_KV_COLS), lambda n: (n, kcol)),
            pl.BlockSpec((blk, SWA_KV_COLS), lambda n: (prev(n), vcol)),
            pl.BlockSpec((blk, SWA_KV_COLS), lambda n: (n, vcol)),
        ],
        out_specs=pl.BlockSpec((blk, SWA_Q_COLS), lambda n: (n, 0)),
        out_shape=jax.ShapeDtypeStruct((m, SWA_Q_COLS), BF16),
        compiler_params=_params(("parallel",)),
        name=name,
    )(sinks, qkv, qkv, qkv, qkv, qkv)


ALIBI_TERMS = 3


TILE_GROUP = 2


def _diff_tile_list(nq):
    return [(qi, qi) for qi in range(nq)] + [(kj, qi) for kj in range(nq) for qi in range(kj + 1, nq)]


def _diff_attn_kernel(slopes_ref, kj_tab_ref, qi_tab_ref, q_ref, k_ref, v_ref, lq1_ref, lk1_ref, lq2_ref,
                      lk2_ref, sg_ref, o_ref, kx_ref, qx_ref, vt_ref, m_ref, l_ref, acc_ref,
                      s0_ref, s1_ref, p0_ref, p1_ref, a0_ref, a1_ref, *, seq, tq, n_tiles, lambda_init):
    h = pl.program_id(1)
    slope = slopes_ref[h]
    nq = seq // tq
    s_refs, p_refs, a_refs = (s0_ref, s1_ref), (p0_ref, p1_ref), (a0_ref, a1_ref)
    low = lax.broadcasted_iota(jnp.int32, (tq, LANES), 1) < DIFF_HEAD_DIM

    def build_tables(c, carry):
        r0 = pl.multiple_of(c * tq, tq)
        pos = (lax.broadcasted_iota(jnp.int32, (tq, LANES), 0) + r0).astype(F32)
        lane = lax.broadcasted_iota(jnp.int32, (tq, LANES), 1)
        a = slope * pos
        a1 = a.astype(BF16).astype(F32)
        a2 = (a - a1).astype(BF16).astype(F32)
        a3 = a - a1 - a2
        k_side = lane < ALIBI_TERMS
        q_side = (lane >= ALIBI_TERMS) & (lane < 2 * ALIBI_TERMS)
        piece = jnp.where(k_side, lane, lane - ALIBI_TERMS)
        pieces = jnp.where(piece == 0, a1, jnp.where(piece == 1, a2, a3))
        kaug = jnp.where(k_side, pieces, jnp.where(q_side, 1.0, 0.0)).astype(BF16)
        qaug = jnp.where(k_side, 1.0, jnp.where(q_side, -pieces, 0.0)).astype(BF16)
        kx_ref[pl.ds(r0, tq), :] = jnp.concatenate([k_ref[pl.ds(r0, tq), :], kaug], axis=1)
        q = q_ref[pl.ds(r0, tq), :] * jnp.asarray(DIFF_HEAD_DIM ** -0.5, BF16)
        zero = jnp.zeros_like(q)
        qx_ref[c] = jnp.concatenate([jnp.concatenate([jnp.where(low, q, zero), qaug], axis=1),
                                     jnp.concatenate([jnp.where(low, zero, q), qaug], axis=1)], axis=0)
        vt_ref[c] = v_ref[pl.ds(r0, tq), :].astype(F32).T.astype(BF16)
        m_ref[c] = jnp.full(m_ref.shape[1:], NEG_INF, F32)
        l_ref[c] = jnp.zeros(l_ref.shape[1:], F32)
        acc_ref[c] = jnp.zeros(acc_ref.shape[1:], F32)
        return carry

    lax.fori_loop(0, nq, build_tables, 0)

    lam =(jnp.exp(jnp.sum(lq1_ref[...] * lk1_ref[...], axis=-1, keepdims=True))
           - jnp.exp(jnp.sum(lq2_ref[...] * lk2_ref[...], axis=-1, keepdims=True))
           + lambda_init)
    key_row = lax.broadcasted_iota(jnp.int32, (tq, 2 * tq), 0)
    query_col = lax.broadcasted_iota(jnp.int32, (tq, 2 * tq), 1)
    causal = key_row <= jnp.where(query_col >= tq, query_col - tq, query_col)
    contract_lanes = (((1,), (1,)), ((), ()))

    def scores(kj, qi):
        k0 = pl.multiple_of(kj * tq, tq)
        return lax.dot_general(kx_ref[pl.ds(k0, tq), :], qx_ref[qi], contract_lanes,
                               preferred_element_type=F32)

    def stage_scores(g, slot):
        for u in range(TILE_GROUP):
            t = g * TILE_GROUP + u
            s_refs[slot][u] = scores(kj_tab_ref[t], qi_tab_ref[t])

    def stage_softmax(g, slot, masked):
        for u in range(TILE_GROUP):
            qi = qi_tab_ref[g * TILE_GROUP + u]
            s = s_refs[slot][u]
            if masked:
                s = jnp.where(causal, s, NEG_INF)
            m_old = m_ref[qi]
            m_new = jnp.maximum(m_old, jnp.max(s, axis=0, keepdims=True))
            p = jnp.exp(s - m_new)
            alpha = jnp.exp(m_old - m_new)
            l_ref[qi] = alpha * l_ref[qi] + jnp.sum(p, axis=0, keepdims=True)
            m_ref[qi] = m_new
            p_refs[slot][u] = p.astype(BF16)
            a_refs[slot][u] = alpha

    def stage_output(g, slot):
        for u in range(TILE_GROUP):
            t = g * TILE_GROUP + u
            qi = qi_tab_ref[t]
            pv = jnp.dot(vt_ref[kj_tab_ref[t]], p_refs[slot][u], preferred_element_type=F32)
            acc_ref[qi] = a_refs[slot][u] * acc_ref[qi] + pv

    n_groups = n_tiles // TILE_GROUP
    n_masked = nq // TILE_GROUP
    assert n_groups >= 2 and n_masked >= 1

    def step(g, slot, n_left):
        if n_left >= 2:
            stage_scores(g + 2, slot)
        if n_left >= 1:
            stage_softmax(g + 1, 1 - slot, masked_group(g + 1))
        stage_output(g, slot)

    masked_group = lambda g: g < n_masked
    stage_scores(0, 0)
    stage_scores(1, 1)
    stage_softmax(0, 0, True)

    n_pairs = (n_groups - 2) // 2
    patterns = [(2 * d + 1 < n_masked, 2 * d + 2 < n_masked) for d in range(n_pairs)]
    start = 0
    while start < n_pairs:
        end = start
        while end < n_pairs and patterns[end] == patterns[start]:
            end += 1
        mask_a, mask_b = patterns[start]

        def body(d, carry, mask_a=mask_a, mask_b=mask_b):
            g = 2 * d
            stage_scores(g + 2, 0)
            stage_softmax(g + 1, 1, mask_a)
            stage_output(g, 0)
            stage_scores(g + 3, 1)
            stage_softmax(g + 2, 0, mask_b)
            stage_output(g + 1, 1)
            return carry

        lax.fori_loop(start, end, body, 0)
        start = end
    for g in range(2 * n_pairs, n_groups):
        step(g, g % 2, n_groups - 1 - g)

    def finish(qi, carry):
        q0 = pl.multiple_of(qi * tq, tq)
        o_t = acc_ref[qi] / l_ref[qi]
        o = (o_t[:, :tq] - lam * o_t[:, tq:]).T
        o = _rms_rows(o, sg_ref[...]) * (1.0 - lambda_init)
        o_ref[pl.ds(q0, tq), :] = o.astype(o_ref.dtype)
        return carry

    lax.fori_loop(0, nq, finish, 0)


def diff_attention(qkv, lq1, lk1, lq2, lk2, subln, *, batch, seq, tq, lambda_init, name):
    m = qkv.shape[0]
    nq = seq // tq
    slopes = jnp.asarray(_alibi_slopes(DIFF_HEADS))
    tiles = _diff_tile_list(nq)
    assert nq % TILE_GROUP == 0 and len(tiles) % TILE_GROUP == 0
    kj_tab = jnp.asarray(np.array([t[0] for t in tiles], np.int32))
    qi_tab = jnp.asarray(np.array([t[1] for t in tiles], np.int32))
    vec = lambda a: a.reshape(1, -1).astype(F32)
    small = lambda width: pl.BlockSpec((1, width), lambda b, h: (0, 0))
    return pl.pallas_call(
        functools.partial(_diff_attn_kernel, seq=seq, tq=tq, n_tiles=len(tiles), lambda_init=lambda_init),
        grid=(batch, DIFF_HEADS),
        in_specs=[
            pl.BlockSpec(memory_space=pltpu.SMEM),
            pl.BlockSpec(memory_space=pltpu.SMEM),
            pl.BlockSpec(memory_space=pltpu.SMEM),
            pl.BlockSpec((seq, DIFF_V_DIM), lambda b, h: (b, h)),
            pl.BlockSpec((seq, DIFF_V_DIM), lambda b, h: (b, DIFF_HEADS + h)),
            pl.BlockSpec((seq, DIFF_V_DIM), lambda b, h: (b, 2 * DIFF_HEADS + h)),
            small(DIFF_HEAD_DIM), small(DIFF_HEAD_DIM), small(DIFF_HEAD_DIM), small(DIFF_HEAD_DIM),
            small(DIFF_V_DIM),
        ],
        out_specs=pl.BlockSpec((seq, DIFF_V_DIM), lambda b, h: (b, h)),
        out_shape=jax.ShapeDtypeStruct((m, DIFF_HEADS * DIFF_V_DIM), BF16),
        scratch_shapes=[
            pltpu.VMEM((seq, 2 * LANES), BF16),
            pltpu.VMEM((nq, 2 * tq, 2 * LANES), BF16),
            pltpu.VMEM((nq, DIFF_V_DIM, tq), BF16),
            pltpu.VMEM((nq, 1, 2 * tq), F32),
            pltpu.VMEM((nq, 1, 2 * tq), F32),
            pltpu.VMEM((nq, DIFF_V_DIM, 2 * tq), F32),
            pltpu.VMEM((TILE_GROUP, tq, 2 * tq), F32), pltpu.VMEM((TILE_GROUP, tq, 2 * tq), F32),
            pltpu.VMEM((TILE_GROUP, tq, 2 * tq), BF16), pltpu.VMEM((TILE_GROUP, tq, 2 * tq), BF16),
            pltpu.VMEM((TILE_GROUP, 1, 2 * tq), F32), pltpu.VMEM((TILE_GROUP, 1, 2 * tq), F32),
        ],
        compiler_params=_params(("parallel", "parallel")),
        name=name,
    )(slopes, kj_tab, qi_tab, qkv, qkv, qkv, vec(lq1), vec(lk1), vec(lq2), vec(lk2), vec(subln))


def _xattn_kernel(h_ref, g_ref, wq_ref, k_ref, v_ref, wo_ref, o_ref):
    x = h_ref[...]
    xn = _rms_rows(x, g_ref[...]).astype(BF16)
    q = jnp.dot(xn, wq_ref[...], preferred_element_type=F32).astype(BF16)
    contract_lanes = (((1,), (1,)), ((), ()))
    outs = []
    for a in range(X_HEADS):
        sl = slice(a * X_HEAD_DIM, (a + 1) * X_HEAD_DIM)
        s = lax.dot_general(q[:, sl], k_ref[:, sl], contract_lanes, preferred_element_type=F32)
        s = s * (X_HEAD_DIM ** -0.5)
        e = jnp.exp(s - jnp.max(s, axis=-1, keepdims=True))
        p = (e / jnp.sum(e, axis=-1, keepdims=True)).astype(BF16)
        outs.append(jnp.dot(p, v_ref[:, sl], preferred_element_type=F32).astype(BF16))
    o = jnp.concatenate(outs, axis=-1)
    o_ref[...] = x + jnp.dot(o, wo_ref[...], preferred_element_type=F32)


def cross_attention(h, g, wq, memkv, wo, *, seq, tm, name):
    m, d = h.shape
    mem_len = memkv.shape[0] // (m // seq)
    per_seq = seq // tm
    return pl.pallas_call(
        _xattn_kernel,
        grid=(m // tm,),
        in_specs=[
            pl.BlockSpec((tm, d), lambda i: (i, 0)),
            pl.BlockSpec((1, d), lambda i: (0, 0)),
            pl.BlockSpec((d, X_WIDTH), lambda i: (0, 0)),
            pl.BlockSpec((mem_len, X_WIDTH), lambda i: (i // per_seq, 0)),
            pl.BlockSpec((mem_len, X_WIDTH), lambda i: (i // per_seq, 1)),
            pl.BlockSpec((X_WIDTH, d), lambda i: (0, 0)),
        ],
        out_specs=pl.BlockSpec((tm, d), lambda i: (i, 0)),
        out_shape=jax.ShapeDtypeStruct((m, d), F32),
        compiler_params=_params(("parallel",)),
        name=name,
    )(h, g.reshape(1, d), wq, memkv, memkv, wo)


CONV_HALO = 8


def _ffn_up_kernel(x_ref, g_ref, wg_ref, wu_ref, cwg_ref, cwu_ref, cbg_ref, cbu_ref, o_ref,
                   xn_ref, ub_ref, carry_ref, *, tm, rc, tiles_per_seq):
    i = pl.program_id(0)
    j = pl.program_id(1)

    @pl.when(j == 0)
    def _():
        _norm_to_bf16(x_ref, g_ref, xn_ref, tm)

    first = (i % tiles_per_seq) == 0

    @pl.when(first)
    def _():
        ub_ref[:, 0:CONV_HALO, :] = jnp.zeros((2, CONV_HALO, ub_ref.shape[2]), F32)

    @pl.when(jnp.logical_not(first))
    def _():
        ub_ref[:, 0:CONV_HALO, :] = carry_ref[j]

    taps = ((cwg_ref[...], cbg_ref[...]), (cwu_ref[...], cbu_ref[...]))
    w_refs = (wg_ref, wu_ref)
    for r in range(tm // rc):
        xs = xn_ref[r * rc:(r + 1) * rc, :]
        lo = CONV_HALO + r * rc
        conv = []
        for t in range(2):
            ub_ref[t, lo:lo + rc, :] = jnp.dot(xs, w_refs[t][...], preferred_element_type=F32)
            cw, cb = taps[t]
            conv.append(cw[2:3] * ub_ref[t, lo:lo + rc, :]
                        + cw[1:2] * ub_ref[t, lo - 1:lo - 1 + rc, :]
                        + cw[0:1] * ub_ref[t, lo - 2:lo - 2 + rc, :]
                        + cb)
        gate, up = conv
        o_ref[r * rc:(r + 1) * rc, :] = (gate / (1.0 + jnp.exp(-gate)) * up).astype(o_ref.dtype)

    carry_ref[j] = ub_ref[:, tm:tm + CONV_HALO, :]


def ffn_up(h, g, w_up, conv_w, conv_b, *, seq, tm, tn, rc, name):
    m, d = h.shape
    nj = D_FF // tn
    return pl.pallas_call(
        functools.partial(_ffn_up_kernel, tm=tm, rc=rc, tiles_per_seq=seq // tm),
        grid=(m // tm, nj),
        in_specs=[
            pl.BlockSpec((tm, d), lambda i, j: (i, 0)),
            pl.BlockSpec((1, d), lambda i, j: (0, 0)),
            pl.BlockSpec((d, tn), lambda i, j: (0, j)),
            pl.BlockSpec((d, tn), lambda i, j: (0, nj + j)),
            pl.BlockSpec((3, tn), lambda i, j: (0, j)),
            pl.BlockSpec((3, tn), lambda i, j: (0, nj + j)),
            pl.BlockSpec((1, tn), lambda i, j: (0, j)),
            pl.BlockSpec((1, tn), lambda i, j: (0, nj + j)),
        ],
        out_specs=pl.BlockSpec((tm, tn), lambda i, j: (i, j)),
        out_shape=jax.ShapeDtypeStruct((m, D_FF), BF16),
        scratch_shapes=[
            pltpu.VMEM((tm, d), BF16),
            pltpu.VMEM((2, CONV_HALO + tm, tn), F32),
            pltpu.VMEM((nj, 2, CONV_HALO, tn), F32),
        ],
        compiler_params=_params(("arbitrary", "arbitrary")),
        name=name,
    )(h, g.reshape(1, d), w_up, w_up, conv_w, conv_w, conv_b.reshape(1, -1), conv_b.reshape(1, -1))


def _final_norm_kernel(x_ref, g_ref, o_ref):
    o_ref[...] = _rms_rows(x_ref[...], g_ref[...])


def final_norm(h, g, *, tm, name):
    m, d = h.shape
    return pl.pallas_call(
        _final_norm_kernel,
        grid=(m // tm,),
        in_specs=[pl.BlockSpec((tm, d), lambda i: (i, 0)), pl.BlockSpec((1, d), lambda i: (0, 0))],
        out_specs=pl.BlockSpec((tm, d), lambda i: (i, 0)),
        out_shape=jax.ShapeDtypeStruct((m, d), F32),
        compiler_params=_params(("parallel",)),
        name=name,
    )(h, g.reshape(1, d))


def kernel(x, mem, norm_mix, norm_cross, norm_mem, norm_ffn, norm_final, swa_w_qkv, swa_w_o, swa_sinks,
           diff_w_qkv, diff_w_o, diff_lambda_q1, diff_lambda_k1, diff_lambda_q2, diff_lambda_k2, diff_subln,
           x_w_q, x_w_kv, x_w_o, ffn_w_up, ffn_conv_w, ffn_conv_b, ffn_w_down):
    batch, seq, d = x.shape
    mem_len = mem.shape[1]
    h = x.reshape(batch * seq, d)
    mem2 = mem.reshape(batch * mem_len, d)
    bf = lambda w: w.astype(BF16)

    for i in range(DEPTH):
        j = i // N_MIXERS
        if i % N_MIXERS == 0:
            qkv = norm_matmul(h, norm_mix[i], bf(swa_w_qkv[j]), tm=1024, tn=1280, name=f"swa_qkv_{i}")
            attn = swa_attention(qkv, swa_sinks[j].astype(F32), seq=seq, name=f"swa_attn_{i}")
            h = matmul_residual(attn, bf(swa_w_o[j]), h, tm=1024, tn=1024, name=f"swa_out_{i}")
        else:
            lambda_init = 0.8 - 0.6 * math.exp(-0.3 * i)
            qkv = norm_matmul(h, norm_mix[i], bf(diff_w_qkv[j]), tm=1024, tn=1024, name=f"diff_qkv_{i}")
            attn = diff_attention(qkv, diff_lambda_q1[j], diff_lambda_k1[j], diff_lambda_q2[j],
                                  diff_lambda_k2[j], diff_subln[j], batch=batch, seq=seq, tq=256,
                                  lambda_init=lambda_init, name=f"diff_attn_{i}")
            h = matmul_residual(attn, bf(diff_w_o[j]), h, tm=1024, tn=1024, name=f"diff_out_{i}")

        memkv = norm_matmul(mem2, norm_mem[i], bf(x_w_kv[i]), tm=batch * mem_len, tn=2 * X_WIDTH,
                            name=f"mem_kv_{i}")
        h = cross_attention(h, norm_cross[i], bf(x_w_q[i]), memkv, bf(x_w_o[i]), seq=seq, tm=512,
                            name=f"xattn_{i}")
        u = ffn_up(h, norm_ffn[i], bf(ffn_w_up[i]), ffn_conv_w[i], ffn_conv_b[i], seq=seq, tm=1024, tn=512,
                   rc=256, name=f"ffn_up_{i}")
        h = matmul_residual(u, bf(ffn_w_down[i]), h, tm=1024, tn=512, name=f"ffn_down_{i}")

    return final_norm(h, norm_final, tm=512, name="final_norm").reshape(batch, seq, d)
```

```python
import functools
import math

import numpy as np
import jax
import jax.numpy as jnp
from jax import lax
from jax.experimental import pallas as pl
from jax.experimental.pallas import tpu as pltpu

F32 = jnp.float32
BF16 = jnp.bfloat16

D_MODEL = 2048
DEPTH = 4
N_MIXERS = 2

SWA_HEADS = 32
SWA_KV_HEADS = 4
SWA_HEAD_DIM = 64
SWA_GROUP = SWA_HEADS // SWA_KV_HEADS
SWA_BLOCK = 128
SWA_Q_COLS = SWA_HEADS * SWA_HEAD_DIM
SWA_KV_COLS = SWA_KV_HEADS * SWA_HEAD_DIM

DIFF_HEADS = 16
DIFF_HEAD_DIM = 64
DIFF_V_DIM = 128

X_HEADS = 4
X_HEAD_DIM = 128
X_WIDTH = X_HEADS * X_HEAD_DIM

D_FF = 5632
RMS_EPS = 1e-6
NEG_INF = -1e30

ALIBI_TERMS = 3
LANES = 128
VMEM_LIMIT = 56 * 1024 * 1024


def _alibi_slopes(n_heads):
    return np.array([2.0 ** (-8.0 * (h + 1) / n_heads) for h in range(n_heads)], dtype=np.float32)


def _params(semantics):
    return pltpu.CompilerParams(dimension_semantics=semantics, vmem_limit_bytes=VMEM_LIMIT)


def _rms_rows(x, g):
    ms = jnp.mean(x * x, axis=-1, keepdims=True)
    return x * lax.rsqrt(ms + RMS_EPS) * g


def _norm_to_bf16(x_ref, g_ref, xn_ref, rows, chunk=256):
    g = g_ref[...]

    def body(c, carry):
        r = pl.multiple_of(c * chunk, chunk)
        xn_ref[pl.ds(r, chunk), :] = _rms_rows(x_ref[pl.ds(r, chunk), :], g).astype(BF16)
        return carry

    lax.fori_loop(0, rows // chunk, body, 0)


def _norm_matmul_kernel(x_ref, g_ref, w_ref, o_ref, xn_ref, *, tm):
    @pl.when(pl.program_id(1) == 0)
    def _():
        _norm_to_bf16(x_ref, g_ref, xn_ref, tm)

    w = w_ref[...].astype(BF16)
    o_ref[...] = jnp.dot(xn_ref[...], w, preferred_element_type=F32).astype(o_ref.dtype)


def norm_matmul(x, g, w_stack, layer, *, tm, tn, name):
    m, d = x.shape
    n = w_stack.shape[2]
    return pl.pallas_call(
        functools.partial(_norm_matmul_kernel, tm=tm),
        grid=(m // tm, n // tn),
        in_specs=[
            pl.BlockSpec((tm, d), lambda i, j: (i, 0)),
            pl.BlockSpec((1, d), lambda i, j: (0, 0)),
            pl.BlockSpec((None, d, tn), lambda i, j: (layer, 0, j)),
        ],
        out_specs=pl.BlockSpec((tm, tn), lambda i, j: (i, j)),
        out_shape=jax.ShapeDtypeStruct((m, n), BF16),
        scratch_shapes=[pltpu.VMEM((tm, d), BF16)],
        compiler_params=_params(("parallel", "arbitrary")),
        name=name,
    )(x, g.reshape(1, d), w_stack)


def _cast_rows(src_ref, dst_ref, chunk):
    def body(c, carry):
        r = pl.multiple_of(c * chunk, chunk)
        dst_ref[pl.ds(r, chunk), :] = src_ref[pl.ds(r, chunk), :].astype(BF16)
        return carry

    lax.fori_loop(0, src_ref.shape[0] // chunk, body, 0)


def _matmul_residual_kernel(a_ref, w_ref, h_ref, o_ref, wb_ref, *, cast_chunk):
    @pl.when(pl.program_id(1) == 0)
    def _():
        _cast_rows(w_ref, wb_ref, cast_chunk)

    o_ref[...] = h_ref[...] + jnp.dot(a_ref[...], wb_ref[...], preferred_element_type=F32)


def matmul_residual(a, w_stack, layer, h, *, tm, tn, cast_chunk, name):
    m, k = a.shape
    n = w_stack.shape[2]
    return pl.pallas_call(
        functools.partial(_matmul_residual_kernel, cast_chunk=cast_chunk),
        grid=(n // tn, m // tm),
        in_specs=[
            pl.BlockSpec((tm, k), lambda j, i: (i, 0)),
            pl.BlockSpec((None, k, tn), lambda j, i: (layer, 0, j)),
            pl.BlockSpec((tm, tn), lambda j, i: (i, j)),
        ],
        out_specs=pl.BlockSpec((tm, tn), lambda j, i: (i, j)),
        out_shape=jax.ShapeDtypeStruct((m, n), F32),
        scratch_shapes=[pltpu.VMEM((k, tn), BF16)],
        compiler_params=_params(("parallel", "arbitrary")),
        name=name,
    )(a, w_stack, h)


def _bf16_pieces(a):
    a = np.asarray(a, np.float32)
    pieces = []
    for _ in range(ALIBI_TERMS):
        p = a.astype(BF16).astype(np.float32)
        pieces.append(p)
        a = a - p
    return pieces


def _swa_alibi_tables():
    t = ALIBI_TERMS
    slopes = _alibi_slopes(SWA_HEADS)
    q = np.arange(SWA_BLOCK, dtype=np.float32)
    qaug = np.zeros((SWA_HEADS, SWA_BLOCK, LANES), np.float32)
    for i, piece in enumerate(_bf16_pieces(slopes)):
        qaug[:, :, i] = piece[:, None]
    for i, piece in enumerate(_bf16_pieces(-slopes[:, None] * (q[None, :] + SWA_BLOCK))):
        qaug[:, :, t + i] = piece
    kaug = np.zeros((2 * SWA_BLOCK, LANES), np.float32)
    kaug[:, :t] = np.arange(2 * SWA_BLOCK, dtype=np.float32)[:, None]
    kaug[:, t:2 * t] = 1.0
    return (jnp.asarray(qaug.reshape(SWA_HEADS * SWA_BLOCK, LANES), BF16), jnp.asarray(kaug, BF16))


def _swa_kernel(sinks_ref, q_ref, kp_ref, kc_ref, vp_ref, vc_ref, qaug_ref, kaug_ref, o_ref, *, blocks_per_seq):
    n = pl.program_id(0)
    first = (n % blocks_per_seq) == 0
    blk = SWA_BLOCK
    half_group = SWA_GROUP // 2
    width = half_group * blk
    key_row = lax.broadcasted_iota(jnp.int32, (2 * blk, width), 0)
    query = lax.broadcasted_iota(jnp.int32, (2 * blk, width), 1) & (blk - 1)
    dist = query + blk - key_row
    first_key = jnp.where(first, blk, 0)
    mask = (dist >= 0) & (dist < blk) & (key_row >= first_key)
    low_q = lax.broadcasted_iota(jnp.int32, (blk, LANES), 1) < SWA_HEAD_DIM
    low_k = lax.broadcasted_iota(jnp.int32, (2 * blk, LANES), 1) < SWA_HEAD_DIM
    kaug = kaug_ref[...]
    contract_lanes = (((1,), (1,)), ((), ()))
    scale = jnp.asarray(SWA_HEAD_DIM ** -0.5, BF16)

    for c in range(SWA_KV_HEADS):
        slab = (c // 2) * LANES
        in_low = c % 2 == 0
        k32 = jnp.concatenate([kp_ref[:, slab:slab + LANES], kc_ref[:, slab:slab + LANES]], axis=0).astype(F32)
        k_sw = pltpu.roll(k32, SWA_HEAD_DIM, 1)
        k_lo = jnp.where(low_k, k32 if in_low else k_sw, 0.0).astype(BF16)
        k_hi = jnp.where(low_k, 0.0, k_sw if in_low else k32).astype(BF16)
        v32 = jnp.concatenate([vp_ref[:, slab:slab + LANES], vc_ref[:, slab:slab + LANES]], axis=0).astype(F32)
        v_rows = slice(0, SWA_HEAD_DIM) if in_low else slice(SWA_HEAD_DIM, LANES)
        v_t = v32.T[v_rows, :].astype(BF16)

        out_t = []
        for half, k_half in enumerate((k_lo, k_hi)):
            kx = jnp.concatenate([k_half, kaug], axis=1)
            heads = [c * SWA_GROUP + 2 * p + half for p in range(half_group)]
            rows = []
            for p, h in enumerate(heads):
                col = (c * half_group + p) * LANES
                qt = q_ref[:, col:col + LANES] * scale
                qt = jnp.where(low_q, qt, jnp.zeros_like(qt)) if half == 0 else jnp.where(low_q, jnp.zeros_like(qt), qt)
                rows.append(jnp.concatenate([qt, qaug_ref[h * blk:(h + 1) * blk, :]], axis=1))
            qx = jnp.concatenate(rows, axis=0)
            s = lax.dot_general(kx, qx, contract_lanes, preferred_element_type=F32)
            s = jnp.where(mask, s, NEG_INF)
            sink = jnp.concatenate([jnp.full((1, blk), sinks_ref[h], F32) for h in heads], axis=1)
            m = jnp.maximum(jnp.max(s, axis=0, keepdims=True), sink)
            e = jnp.exp(s - m)
            denom = jnp.sum(e, axis=0, keepdims=True) + jnp.exp(sink - m)
            out_t.append(jnp.dot(v_t, e.astype(BF16), preferred_element_type=F32) / denom)
        for p in range(half_group):
            col = (c * half_group + p) * LANES
            pair = jnp.concatenate([out_t[0][:, p * blk:(p + 1) * blk], out_t[1][:, p * blk:(p + 1) * blk]], axis=0)
            o_ref[:, col:col + LANES] = pair.T.astype(o_ref.dtype)


def swa_attention(qkv, sinks, *, seq, name):
    m = qkv.shape[0]
    blk = SWA_BLOCK
    bps = seq // blk
    kcol = SWA_Q_COLS // SWA_KV_COLS
    vcol = kcol + 1
    qaug, kaug = _swa_alibi_tables()

    def prev(n):
        return jnp.maximum(n - 1, 0)

    return pl.pallas_call(
        functools.partial(_swa_kernel, blocks_per_seq=bps),
        grid=(m // blk,),
        in_specs=[
            pl.BlockSpec(memory_space=pltpu.SMEM),
            pl.BlockSpec((blk, SWA_Q_COLS), lambda n: (n, 0)),
            pl.BlockSpec((blk, SWA_KV_COLS), lambda n: (prev(n), kcol)),
            pl.BlockSpec((blk, SWA_KV_COLS), lambda n: (n, kcol)),
            pl.BlockSpec((blk, SWA_KV_COLS), lambda n: (prev(n), vcol)),
            pl.BlockSpec((blk, SWA_KV_COLS), lambda n: (n, vcol)),
            pl.BlockSpec(qaug.shape, lambda n: (0, 0)),
            pl.BlockSpec(kaug.shape, lambda n: (0, 0)),
        ],
        out_specs=pl.BlockSpec((blk, SWA_Q_COLS), lambda n: (n, 0)),
        out_shape=jax.ShapeDtypeStruct((m, SWA_Q_COLS), BF16),
        compiler_params=_params(("parallel",)),
        name=name,
    )(sinks, qkv, qkv, qkv, qkv, qkv, qaug, kaug)


TILE_GROUP = 2


def _diff_tile_list(nq):
    return [(qi, qi) for qi in range(nq)] + [(kj, qi) for kj in range(nq) for qi in range(kj + 1, nq)]


def _diff_attn_kernel(slopes_ref, kj_tab_ref, qi_tab_ref, q_ref, k_ref, v_ref, lq1_ref, lk1_ref, lq2_ref,
                      lk2_ref, sg_ref, o_ref, kx_ref, qx_ref, vt_ref, m_ref, l_ref, acc_ref,
                      s0_ref, s1_ref, p0_ref, p1_ref, a0_ref, a1_ref, *, seq, tq, n_tiles, lambda_init):
    h = pl.program_id(1)
    slope = slopes_ref[h]
    nq = seq // tq
    s_refs, p_refs, a_refs = (s0_ref, s1_ref), (p0_ref, p1_ref), (a0_ref, a1_ref)
    low = lax.broadcasted_iota(jnp.int32, (tq, LANES), 1) < DIFF_HEAD_DIM

    def build_tables(c, carry):
        r0 = pl.multiple_of(c * tq, tq)
        pos = (lax.broadcasted_iota(jnp.int32, (tq, LANES), 0) + r0).astype(F32)
        lane = lax.broadcasted_iota(jnp.int32, (tq, LANES), 1)
        a = slope * pos
        a1 = a.astype(BF16).astype(F32)
        a2 = (a - a1).astype(BF16).astype(F32)
        a3 = a - a1 - a2
        k_side = lane < ALIBI_TERMS
        q_side = (lane >= ALIBI_TERMS) & (lane < 2 * ALIBI_TERMS)
        piece = jnp.where(k_side, lane, lane - ALIBI_TERMS)
        pieces = jnp.where(piece == 0, a1, jnp.where(piece == 1, a2, a3))
        kaug = jnp.where(k_side, pieces, jnp.where(q_side, 1.0, 0.0)).astype(BF16)
        qaug = jnp.where(k_side, 1.0, jnp.where(q_side, -pieces, 0.0)).astype(BF16)
        kx_ref[pl.ds(r0, tq), :] = jnp.concatenate([k_ref[pl.ds(r0, tq), :], kaug], axis=1)
        q = q_ref[pl.ds(r0, tq), :] * jnp.asarray(DIFF_HEAD_DIM ** -0.5, BF16)
        zero = jnp.zeros_like(q)
        qx_ref[c] = jnp.concatenate([jnp.concatenate([jnp.where(low, q, zero), qaug], axis=1),
                                     jnp.concatenate([jnp.where(low, zero, q), qaug], axis=1)], axis=0)
        vt_ref[c] = v_ref[pl.ds(r0, tq), :].astype(F32).T.astype(BF16)
        m_ref[c] = jnp.full(m_ref.shape[1:], NEG_INF, F32)
        l_ref[c] = jnp.zeros(l_ref.shape[1:], F32)
        acc_ref[c] = jnp.zeros(acc_ref.shape[1:], F32)
        return carry

    lax.fori_loop(0, nq, build_tables, 0)

    lam =(jnp.exp(jnp.sum(lq1_ref[...] * lk1_ref[...], axis=-1, keepdims=True))
           - jnp.exp(jnp.sum(lq2_ref[...] * lk2_ref[...], axis=-1, keepdims=True))
           + lambda_init)
    key_row = lax.broadcasted_iota(jnp.int32, (tq, 2 * tq), 0)
    query_col = lax.broadcasted_iota(jnp.int32, (tq, 2 * tq), 1)
    causal = key_row <= jnp.where(query_col >= tq, query_col - tq, query_col)
    contract_lanes = (((1,), (1,)), ((), ()))

    def scores(kj, qi):
        k0 = pl.multiple_of(kj * tq, tq)
        return lax.dot_general(kx_ref[pl.ds(k0, tq), :], qx_ref[qi], contract_lanes,
                               preferred_element_type=F32)

    def stage_scores(g, slot):
        for u in range(TILE_GROUP):
            t = g * TILE_GROUP + u
            s_refs[slot][u] = scores(kj_tab_ref[t], qi_tab_ref[t])

    def stage_softmax(g, slot, masked):
        for u in range(TILE_GROUP):
            qi = qi_tab_ref[g * TILE_GROUP + u]
            s = s_refs[slot][u]
            if masked:
                s = jnp.where(causal, s, NEG_INF)
            m_old = m_ref[qi]
            m_new = jnp.maximum(m_old, jnp.max(s, axis=0, keepdims=True))
            p = jnp.exp(s - m_new)
            alpha = jnp.exp(m_old - m_new)
            l_ref[qi] = alpha * l_ref[qi] + jnp.sum(p, axis=0, keepdims=True)
            m_ref[qi] = m_new
            p_refs[slot][u] = p.astype(BF16)
            a_refs[slot][u] = alpha

    def stage_output(g, slot):
        for u in range(TILE_GROUP):
            t = g * TILE_GROUP + u
            qi = qi_tab_ref[t]
            pv = jnp.dot(vt_ref[kj_tab_ref[t]], p_refs[slot][u], preferred_element_type=F32)
            acc_ref[qi] = a_refs[slot][u] * acc_ref[qi] + pv

    n_groups = n_tiles // TILE_GROUP
    n_masked = nq // TILE_GROUP
    assert n_groups >= 2 and n_masked >= 1

    def step(g, slot, n_left):
        if n_left >= 2:
            stage_scores(g + 2, slot)
        if n_left >= 1:
            stage_softmax(g + 1, 1 - slot, masked_group(g + 1))
        stage_output(g, slot)

    masked_group = lambda g: g < n_masked
    stage_scores(0, 0)
    stage_scores(1, 1)
    stage_softmax(0, 0, True)

    n_pairs = (n_groups - 2) // 2
    patterns = [(2 * d + 1 < n_masked, 2 * d + 2 < n_masked) for d in range(n_pairs)]
    start = 0
    while start < n_pairs:
        end = start
        while end < n_pairs and patterns[end] == patterns[start]:
            end += 1
        mask_a, mask_b = patterns[start]

        def body(d, carry, mask_a=mask_a, mask_b=mask_b):
            g = 2 * d
            stage_scores(g + 2, 0)
            stage_softmax(g + 1, 1, mask_a)
            stage_output(g, 0)
            stage_scores(g + 3, 1)
            stage_softmax(g + 2, 0, mask_b)
            stage_output(g + 1, 1)
            return carry

        lax.fori_loop(start, end, body, 0)
        start = end
    for g in range(2 * n_pairs, n_groups):
        step(g, g % 2, n_groups - 1 - g)

    def finish(qi, carry):
        q0 = pl.multiple_of(qi * tq, tq)
        o_t = acc_ref[qi] / l_ref[qi]
        o = (o_t[:, :tq] - lam * o_t[:, tq:]).T
        o = _rms_rows(o, sg_ref[...]) * (1.0 - lambda_init)
        o_ref[pl.ds(q0, tq), :] = o.astype(o_ref.dtype)
        return carry

    lax.fori_loop(0, nq, finish, 0)


def diff_attention(qkv, lq1, lk1, lq2, lk2, subln, *, batch, seq, tq, lambda_init, name):
    m = qkv.shape[0]
    nq = seq // tq
    slopes = jnp.asarray(_alibi_slopes(DIFF_HEADS))
    tiles = _diff_tile_list(nq)
    assert nq % TILE_GROUP == 0 and len(tiles) % TILE_GROUP == 0
    kj_tab = jnp.asarray(np.array([t[0] for t in tiles], np.int32))
    qi_tab = jnp.asarray(np.array([t[1] for t in tiles], np.int32))
    vec = lambda a: a.reshape(1, -1).astype(F32)
    small = lambda width: pl.BlockSpec((1, width), lambda b, h: (0, 0))
    return pl.pallas_call(
        functools.partial(_diff_attn_kernel, seq=seq, tq=tq, n_tiles=len(tiles), lambda_init=lambda_init),
        grid=(batch, DIFF_HEADS),
        in_specs=[
            pl.BlockSpec(memory_space=pltpu.SMEM),
            pl.BlockSpec(memory_space=pltpu.SMEM),
            pl.BlockSpec(memory_space=pltpu.SMEM),
            pl.BlockSpec((seq, DIFF_V_DIM), lambda b, h: (b, h)),
            pl.BlockSpec((seq, DIFF_V_DIM), lambda b, h: (b, DIFF_HEADS + h)),
            pl.BlockSpec((seq, DIFF_V_DIM), lambda b, h: (b, 2 * DIFF_HEADS + h)),
            small(DIFF_HEAD_DIM), small(DIFF_HEAD_DIM), small(DIFF_HEAD_DIM), small(DIFF_HEAD_DIM),
            small(DIFF_V_DIM),
        ],
        out_specs=pl.BlockSpec((seq, DIFF_V_DIM), lambda b, h: (b, h)),
        out_shape=jax.ShapeDtypeStruct((m, DIFF_HEADS * DIFF_V_DIM), BF16),
        scratch_shapes=[
            pltpu.VMEM((seq, 2 * LANES), BF16),
            pltpu.VMEM((nq, 2 * tq, 2 * LANES), BF16),
            pltpu.VMEM((nq, DIFF_V_DIM, tq), BF16),
            pltpu.VMEM((nq, 1, 2 * tq), F32),
            pltpu.VMEM((nq, 1, 2 * tq), F32),
            pltpu.VMEM((nq, DIFF_V_DIM, 2 * tq), F32),
            pltpu.VMEM((TILE_GROUP, tq, 2 * tq), F32), pltpu.VMEM((TILE_GROUP, tq, 2 * tq), F32),
            pltpu.VMEM((TILE_GROUP, tq, 2 * tq), BF16), pltpu.VMEM((TILE_GROUP, tq, 2 * tq), BF16),
            pltpu.VMEM((TILE_GROUP, 1, 2 * tq), F32), pltpu.VMEM((TILE_GROUP, 1, 2 * tq), F32),
        ],
        compiler_params=_params(("parallel", "parallel")),
        name=name,
    )(slopes, kj_tab, qi_tab, qkv, qkv, qkv, vec(lq1), vec(lk1), vec(lq2), vec(lk2), vec(subln))


def _xattn_kernel(h_ref, g_ref, gn_ref, wq_ref, k_ref, v_ref, wo_ref, o_ref, on_ref, wqb_ref, wob_ref):
    @pl.when(pl.program_id(0) == 0)
    def _():
        _cast_rows(wq_ref, wqb_ref, 256)
        _cast_rows(wo_ref, wob_ref, 256)

    x = h_ref[...]
    xn = _rms_rows(x, g_ref[...]).astype(BF16)
    q = jnp.dot(xn, wqb_ref[...], preferred_element_type=F32).astype(BF16)
    contract_lanes = (((1,), (1,)), ((), ()))
    outs = []
    for a in range(X_HEADS):
        sl = slice(a * X_HEAD_DIM, (a + 1) * X_HEAD_DIM)
        s = lax.dot_general(q[:, sl], k_ref[:, sl], contract_lanes, preferred_element_type=F32)
        s = s * (X_HEAD_DIM ** -0.5)
        e = jnp.exp(s - jnp.max(s, axis=-1, keepdims=True))
        p = (e / jnp.sum(e, axis=-1, keepdims=True)).astype(BF16)
        outs.append(jnp.dot(p, v_ref[:, sl], preferred_element_type=F32).astype(BF16))
    o = jnp.concatenate(outs, axis=-1)
    y = x + jnp.dot(o, wob_ref[...], preferred_element_type=F32)
    o_ref[...] = y
    on_ref[...] = _rms_rows(y, gn_ref[...]).astype(on_ref.dtype)


def cross_attention(h, g, g_next, wq_stack, memkv, wo_stack, layer, *, seq, tm, name):
    m, d = h.shape
    mem_len = memkv.shape[0] // (m // seq)
    per_seq = seq // tm
    return pl.pallas_call(
        _xattn_kernel,
        grid=(m // tm,),
        in_specs=[
            pl.BlockSpec((tm, d), lambda i: (i, 0)),
            pl.BlockSpec((1, d), lambda i: (0, 0)),
            pl.BlockSpec((1, d), lambda i: (0, 0)),
            pl.BlockSpec((None, d, X_WIDTH), lambda i: (layer, 0, 0)),
            pl.BlockSpec((mem_len, X_WIDTH), lambda i: (i // per_seq, 0)),
            pl.BlockSpec((mem_len, X_WIDTH), lambda i: (i // per_seq, 1)),
            pl.BlockSpec((None, X_WIDTH, d), lambda i: (layer, 0, 0)),
        ],
        out_specs=[pl.BlockSpec((tm, d), lambda i: (i, 0)), pl.BlockSpec((tm, d), lambda i: (i, 0))],
        out_shape=[jax.ShapeDtypeStruct((m, d), F32), jax.ShapeDtypeStruct((m, d), BF16)],
        scratch_shapes=[pltpu.VMEM((d, X_WIDTH), BF16), pltpu.VMEM((X_WIDTH, d), BF16)],
        compiler_params=_params(("arbitrary",)),
        name=name,
    )(h, g.reshape(1, d), g_next.reshape(1, d), wq_stack, memkv, memkv, wo_stack)


CONV_HALO = 8


def _ffn_up_kernel(xn_ref, wg_ref, wu_ref, cwg_ref, cwu_ref, cbg_ref, cbu_ref, o_ref,
                   wgb_ref, wub_ref, ub_ref, *, tm, rc, tiles_per_seq):
    i = pl.program_id(1)

    @pl.when(i == 0)
    def _():
        _cast_rows(wg_ref, wgb_ref, 256)
        _cast_rows(wu_ref, wub_ref, 256)

    @pl.when((i % tiles_per_seq) == 0)
    def _():
        ub_ref[:, 0:CONV_HALO, :] = jnp.zeros((2, CONV_HALO, ub_ref.shape[2]), F32)

    taps = ((cwg_ref[...], cbg_ref[...]), (cwu_ref[...], cbu_ref[...]))
    w_refs = (wgb_ref, wub_ref)
    for r in range(tm // rc):
        xs = xn_ref[r * rc:(r + 1) * rc, :]
        lo = CONV_HALO + r * rc
        conv = []
        for t in range(2):
            ub_ref[t, lo:lo + rc, :] = jnp.dot(xs, w_refs[t][...], preferred_element_type=F32)
            cw, cb = taps[t]
            conv.append(cw[2:3] * ub_ref[t, lo:lo + rc, :]
                        + cw[1:2] * ub_ref[t, lo - 1:lo - 1 + rc, :]
                        + cw[0:1] * ub_ref[t, lo - 2:lo - 2 + rc, :]
                        + cb)
        gate, up = conv
        o_ref[r * rc:(r + 1) * rc, :] = (gate / (1.0 + jnp.exp(-gate)) * up).astype(o_ref.dtype)

    ub_ref[:, 0:CONV_HALO, :] = ub_ref[:, tm:tm + CONV_HALO, :]


def ffn_up(xn, w_up_stack, conv_w_stack, conv_b_stack, layer, *, seq, tm, tn, rc, name):
    m, d = xn.shape
    nj = D_FF // tn
    conv_b_stack = conv_b_stack.reshape(conv_b_stack.shape[0], 1, -1)
    taps = conv_w_stack.shape[1]
    return pl.pallas_call(
        functools.partial(_ffn_up_kernel, tm=tm, rc=rc, tiles_per_seq=seq // tm),
        grid=(nj, m // tm),
        in_specs=[
            pl.BlockSpec((tm, d), lambda j, i: (i, 0)),
            pl.BlockSpec((None, d, tn), lambda j, i: (layer, 0, j)),
            pl.BlockSpec((None, d, tn), lambda j, i: (layer, 0, nj + j)),
            pl.BlockSpec((None, taps, tn), lambda j, i: (layer, 0, j)),
            pl.BlockSpec((None, taps, tn), lambda j, i: (layer, 0, nj + j)),
            pl.BlockSpec((None, 1, tn), lambda j, i: (layer, 0, j)),
            pl.BlockSpec((None, 1, tn), lambda j, i: (layer, 0, nj + j)),
        ],
        out_specs=pl.BlockSpec((tm, tn), lambda j, i: (i, j)),
        out_shape=jax.ShapeDtypeStruct((m, D_FF), BF16),
        scratch_shapes=[
            pltpu.VMEM((d, tn), BF16),
            pltpu.VMEM((d, tn), BF16),
            pltpu.VMEM((2, CONV_HALO + tm, tn), F32),
        ],
        compiler_params=_params(("parallel", "arbitrary")),
        name=name,
    )(xn, w_up_stack, w_up_stack, conv_w_stack, conv_w_stack, conv_b_stack, conv_b_stack)


def _final_norm_kernel(x_ref, g_ref, o_ref):
    o_ref[...] = _rms_rows(x_ref[...], g_ref[...])


def final_norm(h, g, *, tm, name):
    m, d = h.shape
    return pl.pallas_call(
        _final_norm_kernel,
        grid=(m // tm,),
        in_specs=[pl.BlockSpec((tm, d), lambda i: (i, 0)), pl.BlockSpec((1, d), lambda i: (0, 0))],
        out_specs=pl.BlockSpec((tm, d), lambda i: (i, 0)),
        out_shape=jax.ShapeDtypeStruct((m, d), F32),
        compiler_params=_params(("parallel",)),
        name=name,
    )(h, g.reshape(1, d))


def kernel(x, mem, norm_mix, norm_cross, norm_mem, norm_ffn, norm_final, swa_w_qkv, swa_w_o, swa_sinks,
           diff_w_qkv, diff_w_o, diff_lambda_q1, diff_lambda_k1, diff_lambda_q2, diff_lambda_k2, diff_subln,
           x_w_q, x_w_kv, x_w_o, ffn_w_up, ffn_conv_w, ffn_conv_b, ffn_w_down):
    batch, seq, d = x.shape
    mem_len = mem.shape[1]
    h = x.reshape(batch * seq, d)
    mem2 = mem.reshape(batch * mem_len, d)

    for i in range(DEPTH):
        j = i // N_MIXERS
        if i % N_MIXERS == 0:
            qkv = norm_matmul(h, norm_mix[i], swa_w_qkv, j, tm=1024, tn=1280, name=f"swa_qkv_{i}")
            attn = swa_attention(qkv, swa_sinks[j].astype(F32), seq=seq, name=f"swa_attn_{i}")
            h = matmul_residual(attn, swa_w_o, j, h, tm=1024, tn=1024, cast_chunk=256, name=f"swa_out_{i}")
        else:
            lambda_init = 0.8 - 0.6 * math.exp(-0.3 * i)
            qkv = norm_matmul(h, norm_mix[i], diff_w_qkv, j, tm=1024, tn=1024, name=f"diff_qkv_{i}")
            attn = diff_attention(qkv, diff_lambda_q1[j], diff_lambda_k1[j], diff_lambda_q2[j],
                                  diff_lambda_k2[j], diff_subln[j], batch=batch, seq=seq, tq=256,
                                  lambda_init=lambda_init, name=f"diff_attn_{i}")
            h = matmul_residual(attn, diff_w_o, j, h, tm=1024, tn=1024, cast_chunk=256, name=f"diff_out_{i}")

        memkv = norm_matmul(mem2, norm_mem[i], x_w_kv, i, tm=batch * mem_len, tn=2 * X_WIDTH, name=f"mem_kv_{i}")
        h, hn = cross_attention(h, norm_cross[i], norm_ffn[i], x_w_q, memkv, x_w_o, i, seq=seq, tm=512,
                                name=f"xattn_{i}")
        u = ffn_up(hn, ffn_w_up, ffn_conv_w, ffn_conv_b, i, seq=seq, tm=1024, tn=512, rc=256, name=f"ffn_up_{i}")
        h = matmul_residual(u, ffn_w_down, i, h, tm=512, tn=512, cast_chunk=704, name=f"ffn_down_{i}")

    return final_norm(h, norm_final, tm=512, name="final_norm").reshape(batch, seq, d)
```
